```python
import math
import jax, jax.numpy as jnp
from jax import lax
import numpy as np

D_MODEL = 1024
BATCH = 16
SEQ = 2048
DEPTH = 2

HEAD_DIM = 64
NORM_EPS = 1e-6
A_HEADS = 8
A_WIDTH = A_HEADS * HEAD_DIM
LORA_W = 64
LORA_A = 64
LNX_EPS = 64e-5
B_GROUPS = 8
B_GROUP_DIM = 64
B_WIDTH = B_GROUPS * B_GROUP_DIM
CHUNK = 128
GMLP_LN_EPS = 1e-5
EVEN_MIX = A_WIDTH + B_WIDTH
SHIFT_W = 3 * A_WIDTH + LORA_W + LORA_A
EVEN_IN = SHIFT_W + 2 * B_WIDTH + EVEN_MIX
C_HEADS = 16
C_KV_HEADS = 2
C_GROUP = C_HEADS // C_KV_HEADS
WINDOW = 128
ODD_MIX = C_HEADS * HEAD_DIM
KV_W = C_KV_HEADS * HEAD_DIM
ODD_IN = ODD_MIX + 2 * KV_W + ODD_MIX
N_EVEN = (DEPTH + 1) // 2
N_ODD = DEPTH // 2

kernel_name = "hybrid_rwkv7_gmlp_swa_adaln"


def rmsnorm(x, g, eps=NORM_EPS):
    xf = x.astype(jnp.float32)
    r = lax.rsqrt(jnp.mean(xf * xf, axis=-1, keepdims=True) + eps)
    return (xf * r).astype(x.dtype) * g


def layernorm(x, g, b, eps):
    xf = x.astype(jnp.float32)
    mu = jnp.mean(xf, axis=-1, keepdims=True)
    var = jnp.mean(jnp.square(xf - mu), axis=-1, keepdims=True)
    return ((xf - mu) * lax.rsqrt(var + eps)).astype(x.dtype) * g + b


def token_shift(p):
    return jnp.pad(p, ((0, 0), (1, 0), (0, 0)))[:, :-1]


def alibi_slopes(n_heads):
    return jnp.asarray(2.0 ** (-8.0 * np.arange(1, n_heads + 1, dtype=np.float32) / n_heads), jnp.float32)


def rwkv7_time_mix(p, mu, w0, w_up, a0, a_up, k_k, k_a, r_k, lnx_g, lnx_b):
    bsz, t_len, _ = p.shape
    p = p + mu * (token_shift(p) - p)
    r, k, v, w_lo, a_lo = jnp.split(
        p, [A_WIDTH, 2 * A_WIDTH, 3 * A_WIDTH, 3 * A_WIDTH + LORA_W], axis=-1)
    w = -jax.nn.softplus(-(w0 + jnp.tanh(w_lo) @ w_up)) - 0.5
    decay = jnp.exp(-jnp.exp(w.astype(jnp.float32))).astype(p.dtype)
    a = jax.nn.sigmoid(a0 + a_lo @ a_up)

    def heads(t):
        return t.reshape(bsz, t_len, A_HEADS, HEAD_DIM)

    kk = heads(k * k_k).astype(jnp.float32)
    kk = (kk / jnp.maximum(jnp.sqrt(jnp.sum(kk * kk, axis=-1, keepdims=True)), 1e-12)).astype(p.dtype)
    k = k * (1 + (a - 1) * k_a)
    r, k, v, decay, a = heads(r), heads(k), heads(v), heads(decay), heads(a)

    def step(S, inp):
        r_t, w_t, k_t, v_t, kk_t, a_t = inp
        sa = jnp.einsum('bhvk,bhk->bhv', S, -kk_t)
        S = (S * w_t[:, :, None, :]
             + sa[..., None] * (kk_t * a_t)[:, :, None, :]
             + v_t[..., None] * k_t[:, :, None, :])
        y_t = jnp.einsum('bhvk,bhk->bhv', S, r_t)
        return S, y_t

    xs = tuple(jnp.moveaxis(t, 1, 0) for t in (r, decay, k, v, kk, a))
    S0 = jnp.zeros((bsz, A_HEADS, HEAD_DIM, HEAD_DIM), jnp.float32)
    _, y = lax.scan(step, S0, xs)
    y = jnp.moveaxis(y, 0, 1).astype(p.dtype)
    y = layernorm(y, lnx_g.reshape(A_HEADS, HEAD_DIM), lnx_b.reshape(A_HEADS, HEAD_DIM), LNX_EPS)
    y = y + jnp.sum(r * k * r_k, axis=-1, keepdims=True) * v
    return y.reshape(bsz, t_len, A_WIDTH)


def chunked_spatial_gate(u, v, ln_g, ln_b, w_s, b_s):
    bsz, t_len, _ = u.shape
    n_chunks = t_len // CHUNK
    v = layernorm(v.reshape(bsz, t_len, B_GROUPS, B_GROUP_DIM),
                  ln_g.reshape(B_GROUPS, B_GROUP_DIM), ln_b.reshape(B_GROUPS, B_GROUP_DIM), GMLP_LN_EPS)
    v = v.reshape(bsz, n_chunks, CHUNK, B_GROUPS, B_GROUP_DIM)
    causal = jnp.tril(jnp.ones((CHUNK, CHUNK), dtype=bool))
    w = jnp.where(causal[None], w_s, jnp.zeros_like(w_s))
    mixed = jnp.einsum('gts,bcsgd->bctgd', w, v) + b_s.T[None, None, :, :, None]
    return u * mixed.reshape(bsz, t_len, B_WIDTH)


def even_mixer(h, w_in, mu, w0, w_up, a0, a_up, k_k, k_a, r_k, lnx_g, lnx_b,
               sg_ln_g, sg_ln_b, sg_w, sg_b, w_out):
    p = h @ w_in
    p_a, u, v, z = jnp.split(p, [SHIFT_W, SHIFT_W + B_WIDTH, SHIFT_W + 2 * B_WIDTH], axis=-1)
    y_a = rwkv7_time_mix(p_a, mu, w0, w_up, a0, a_up, k_k, k_a, r_k, lnx_g, lnx_b)
    y_b = chunked_spatial_gate(u, v, sg_ln_g, sg_ln_b, sg_w, sg_b)
    y = jnp.concatenate([y_a, y_b], axis=-1) * jax.nn.silu(z)
    return y @ w_out


def swa_mixer(h, w_in, sinks, w_out):
    bsz, t_len, _ = h.shape
    nb = t_len // WINDOW
    p = h @ w_in
    q, k, v, z = jnp.split(p, [ODD_MIX, ODD_MIX + KV_W, ODD_MIX + 2 * KV_W], axis=-1)
    q = q.reshape(bsz, nb, WINDOW, C_KV_HEADS, C_GROUP, HEAD_DIM)

    def band(t):
        t = jnp.pad(t.reshape(bsz, t_len, C_KV_HEADS, HEAD_DIM), ((0, 0), (WINDOW, 0), (0, 0), (0, 0)))
        t = t.reshape(bsz, nb + 1, WINDOW, C_KV_HEADS, HEAD_DIM)
        return jnp.concatenate([t[:, :-1], t[:, 1:]], axis=2)

    k_band, v_band = band(k), band(v)
    scores = jnp.einsum('bnqhgd,bnkhd->bnhgqk', q, k_band).astype(jnp.float32) * (HEAD_DIM ** -0.5)
    qi = jnp.arange(WINDOW)[:, None]
    kj = jnp.arange(2 * WINDOW)[None, :]
    dist = qi + WINDOW - kj
    key_pos = (jnp.arange(nb)[:, None, None] - 1) * WINDOW + kj[None]
    valid = (dist >= 0)[None] & (dist < WINDOW)[None] & (key_pos >= 0)
    slopes = alibi_slopes(C_HEADS).reshape(C_KV_HEADS, C_GROUP)
    scores = scores - slopes[None, None, :, :, None, None] * dist.astype(jnp.float32)
    scores = jnp.where(valid[None, :, None, None], scores, -jnp.inf)
    sink = sinks.reshape(C_KV_HEADS, C_GROUP).astype(jnp.float32)[None, None, :, :, None, None]
    m = jnp.maximum(jnp.max(scores, axis=-1, keepdims=True), sink)
    e = jnp.exp(scores - m)
    probs = e / (jnp.sum(e, axis=-1, keepdims=True) + jnp.exp(sink - m))
    out = jnp.einsum('bnhgqk,bnkhd->bnqhgd', probs.astype(v_band.dtype), v_band)
    y = out.reshape(bsz, t_len, ODD_MIX) * jax.nn.silu(z)
    return y @ w_out


def setup_inputs(seed: int = 0) -> dict:
    key = jax.random.key(seed)
    ks = jax.random.split(key, 32)
    f32 = jnp.float32
    nrm = lambda k, shape, s: (jax.random.normal(k, shape, f32) * s).astype(f32)
    D = D_MODEL
    return {
        "x": nrm(ks[0], (BATCH, SEQ, D), 1.0),
        "c": nrm(ks[1], (BATCH, D), 1.0),
        "ada_w": nrm(ks[2], (DEPTH, D, 3 * D), 0.2 * D ** -0.5),
        "ada_b": nrm(ks[3], (DEPTH, 3 * D), 0.1),
        "norm_g": 1.0 + nrm(ks[4], (DEPTH, D), 0.02),
        "e_w_in": nrm(ks[5], (N_EVEN, D, EVEN_IN), D ** -0.5),
        "e_mu": jax.random.uniform(ks[6], (N_EVEN, SHIFT_W), f32),
        "e_w0": -0.5 + nrm(ks[7], (N_EVEN, A_WIDTH), 0.5),
        "e_w_up": nrm(ks[8], (N_EVEN, LORA_W, A_WIDTH), 0.5 * LORA_W ** -0.5),
        "e_a0": nrm(ks[9], (N_EVEN, A_WIDTH), 0.5),
        "e_a_up": nrm(ks[10], (N_EVEN, LORA_A, A_WIDTH), 0.5 * LORA_A ** -0.5),
        "e_k_k": 0.85 + nrm(ks[11], (N_EVEN, A_WIDTH), 0.05),
        "e_k_a": 1.0 + nrm(ks[12], (N_EVEN, A_WIDTH), 0.05),
        "e_r_k": nrm(ks[13], (N_EVEN, A_HEADS, HEAD_DIM), 0.1),
        "e_lnx_g": 1.0 + nrm(ks[14], (N_EVEN, A_WIDTH), 0.02),
        "e_lnx_b": nrm(ks[15], (N_EVEN, A_WIDTH), 0.02),
        "e_sg_ln_g": 1.0 + nrm(ks[16], (N_EVEN, B_WIDTH), 0.02),
        "e_sg_ln_b": nrm(ks[17], (N_EVEN, B_WIDTH), 0.02),
        "e_sg_w": nrm(ks[18], (N_EVEN, B_GROUPS, CHUNK, CHUNK), CHUNK ** -0.5),
        "e_sg_b": 1.0 + nrm(ks[19], (N_EVEN, B_GROUPS, CHUNK), 0.02),
        "e_w_out": nrm(ks[20], (N_EVEN, EVEN_MIX, D), EVEN_MIX ** -0.5),
        "o_w_in": nrm(ks[21], (N_ODD, D, ODD_IN), D ** -0.5),
        "o_sinks": nrm(ks[22], (N_ODD, C_HEADS), 0.5),
        "o_w_out": nrm(ks[23], (N_ODD, ODD_MIX, D), ODD_MIX ** -0.5),
        "final_g": 1.0 + nrm(ks[24], (D,), 0.02),
    }


def reference(x, c, ada_w, ada_b, norm_g, e_w_in, e_mu, e_w0, e_w_up, e_a0, e_a_up,
              e_k_k, e_k_a, e_r_k, e_lnx_g, e_lnx_b, e_sg_ln_g, e_sg_ln_b, e_sg_w, e_sg_b,
              e_w_out, o_w_in, o_sinks, o_w_out, final_g):
    cond = jax.nn.silu(c)
    for i in range(DEPTH):
        mod = cond @ ada_w[i] + ada_b[i]
        shift, scale, gate = jnp.split(mod[:, None, :], 3, axis=-1)
        h = rmsnorm(x, norm_g[i]) * (1 + scale) + shift
        j = i // 2
        if i % 2 == 0:
            y = even_mixer(h, e_w_in[j], e_mu[j], e_w0[j], e_w_up[j], e_a0[j], e_a_up[j],
                           e_k_k[j], e_k_a[j], e_r_k[j], e_lnx_g[j], e_lnx_b[j],
                           e_sg_ln_g[j], e_sg_ln_b[j], e_sg_w[j], e_sg_b[j], e_w_out[j])
        else:
            y = swa_mixer(h, o_w_in[j], o_sinks[j], o_w_out[j])
        x = x + gate * y
    return rmsnorm(x, final_g)
```

```python
import functools
import math

import jax
import jax.numpy as jnp
import numpy as np
from jax import lax
from jax.experimental import pallas as pl
from jax.experimental.pallas import tpu as pltpu

F32 = jnp.float32
BF16 = jnp.bfloat16

HEAD_DIM = 64
LANES = 128
NORM_EPS = 1e-6
LNX_EPS = 64e-5
GMLP_LN_EPS = 1e-5
A_WIDTH = 512
B_WIDTH = 512
LORA = 64
SHIFT_W = 3 * A_WIDTH + 2 * LORA
CHUNK = 128
WINDOW = 128
RCHUNK = 64
C_HEADS = 16
C_KV_HEADS = 2
VMEM_LIMIT = 56 * 1024 * 1024


def _cparams(sem):
    return pltpu.CompilerParams(dimension_semantics=sem, vmem_limit_bytes=VMEM_LIMIT)


def _split_bf16(a):
    hi = a.astype(BF16)
    lo = (a - hi.astype(F32)).astype(BF16)
    return hi, lo


def _mm(a, b, passes=1, nt=False):
    dn = (((1,), (1 if nt else 0,)), ((), ()))
    if passes == 1:
        return lax.dot_general(a.astype(BF16), b.astype(BF16), dn, preferred_element_type=F32)
    a_hi, a_lo = _split_bf16(a)
    b_hi, b_lo = _split_bf16(b)
    lhs = jnp.concatenate([a_hi, a_lo, a_hi], axis=1)
    rhs = jnp.concatenate([b_hi, b_hi, b_lo], axis=1 if nt else 0)
    return lax.dot_general(lhs, rhs, dn, preferred_element_type=F32)


def _mm_exact_rhs(a, b_bf16, pieces=2):
    parts = []
    rem = a
    for _ in range(pieces):
        p = rem.astype(BF16)
        parts.append(p)
        rem = rem - p.astype(F32)
    lhs = jnp.concatenate(parts, axis=1)
    rhs = jnp.concatenate([b_bf16] * pieces, axis=0)
    return jnp.dot(lhs, rhs, preferred_element_type=F32)


def _silu(x):
    return x / (1.0 + jnp.exp(-x))


def _group_ones(width):
    g = np.arange(width) // HEAD_DIM
    return jnp.asarray((g[:, None] == g[None, :]).astype(np.float32), BF16)


def _adaln_kernel(c_ref, w_ref, b_ref, o_ref):
    cond = _silu(c_ref[...])
    o_ref[0] = _mm(cond, w_ref[0], passes=3) + b_ref[0]


def adaln_call(c, ada_w, ada_b):
    depth, d, d3 = ada_w.shape
    bsz = c.shape[0]
    tn = d
    return pl.pallas_call(
        _adaln_kernel,
        out_shape=jax.ShapeDtypeStruct((depth, bsz, d3), F32),
        grid=(depth, d3 // tn),
        in_specs=[
            pl.BlockSpec((bsz, d), lambda i, j: (0, 0)),
            pl.BlockSpec((1, d, tn), lambda i, j: (i, 0, j)),
            pl.BlockSpec((1, 1, tn), lambda i, j: (i, 0, j)),
        ],
        out_specs=pl.BlockSpec((1, bsz, tn), lambda i, j: (i, 0, j)),
        compiler_params=_cparams(("arbitrary", "arbitrary")),
        name="adaln",
    )(c, ada_w, ada_b.reshape(depth, 1, d3))


def _modulated_norm(x, g, scale, shift):
    r = lax.rsqrt(jnp.mean(x * x, axis=-1, keepdims=True) + NORM_EPS)
    return (x * r) * g * (1.0 + scale) + shift


def _in_proj_kernel(x_ref, g_ref, scale_ref, shift_ref, w_ref, *o_refs, widths):
    h = _modulated_norm(x_ref[0], g_ref[...], scale_ref[0], shift_ref[0]).astype(BF16)
    off = 0
    for o_ref, wd in zip(o_refs, widths):
        o_ref[0] = jnp.dot(h, w_ref[:, off:off + wd], preferred_element_type=F32).astype(o_ref.dtype)
        off += wd


def in_proj_call(x, g, scale, shift, w_bf16, widths, tm=512):
    bsz, t_len, d = x.shape
    n_out = w_bf16.shape[1]
    tm = min(tm, t_len)
    assert sum(widths) == n_out and t_len % tm == 0
    vec = pl.BlockSpec((1, 1, d), lambda b, i: (b, 0, 0))
    return pl.pallas_call(
        functools.partial(_in_proj_kernel, widths=widths),
        out_shape=[jax.ShapeDtypeStruct((bsz, t_len, wd), F32) for wd in widths],
        grid=(bsz, t_len // tm),
        in_specs=[
            pl.BlockSpec((1, tm, d), lambda b, i: (b, i, 0)),
            pl.BlockSpec((1, d), lambda b, i: (0, 0)),
            vec, vec,
            pl.BlockSpec((d, n_out), lambda b, i: (0, 0)),
        ],
        out_specs=[pl.BlockSpec((1, tm, wd), lambda b, i: (b, i, 0)) for wd in widths],
        compiler_params=_cparams(("arbitrary", "arbitrary")),
        name="in_proj",
    )(x, g.reshape(1, d), scale, shift, w_bf16)


def _stack_heads(z):
    first = lax.broadcasted_iota(jnp.int32, z.shape, 1) < HEAD_DIM
    zero = jnp.zeros_like(z)
    return jnp.concatenate([jnp.where(first, z, zero), jnp.where(first, zero, z)], axis=0)


def _rwkv_kernel(p_ref, z_ref, mu_ref, w0_ref, a0_ref, lora_ref, kk_ref, ka_ref, rk_ref,
                 lng_ref, lnb_ref, gones_ref, o_ref, state_ref, prev_ref):
    c_len = RCHUNK
    n_pairs = A_WIDTH // LANES

    @pl.when(pl.program_id(1) == 0)
    def _():
        state_ref[...] = jnp.zeros_like(state_ref)
        prev_ref[...] = jnp.zeros_like(prev_ref)

    p = p_ref[0]
    row = lax.broadcasted_iota(jnp.int32, p.shape, 0)
    shifted = jnp.where(row == 0, prev_ref[0:1, :], pltpu.roll(p, 1, 0))
    prev_ref[0:1, :] = p[c_len - 1:c_len, :]
    ps = p + mu_ref[...] * (shifted - p)

    r = ps[:, 0:A_WIDTH]
    k = ps[:, A_WIDTH:2 * A_WIDTH]
    v = ps[:, 2 * A_WIDTH:3 * A_WIDTH]
    lo = ps[:, 3 * A_WIDTH:SHIFT_W]
    lane = lax.broadcasted_iota(jnp.int32, lo.shape, 1)
    lo = jnp.where(lane < LORA, jnp.tanh(lo), lo)
    up = _mm(lo, lora_ref[...], passes=3)
    xw = -(w0_ref[...] + up[:, :A_WIDTH])
    softplus = jnp.maximum(xw, 0.0) + jnp.log(1.0 + jnp.exp(-jnp.abs(xw)))
    logw = -jnp.exp(-softplus - 0.5)
    a = 1.0 / (1.0 + jnp.exp(-(a0_ref[...] + up[:, A_WIDTH:])))

    gones = gones_ref[...]
    kk = k * kk_ref[...]
    kk = kk / jnp.maximum(jnp.sqrt(_mm_exact_rhs(kk * kk, gones)), 1e-12)
    k = k * (1.0 + (a - 1.0) * ka_ref[...])

    ti = lax.broadcasted_iota(jnp.int32, (c_len, c_len), 0)
    si = lax.broadcasted_iota(jnp.int32, (c_len, c_len), 1)
    tri = jnp.where(si <= ti, 1.0, 0.0).astype(BF16)
    parts = []
    rem = logw
    for _ in range(3):
        piece = rem.astype(BF16)
        parts.append(piece)
        rem = rem - piece.astype(F32)
    cum = jnp.dot(jnp.concatenate([tri] * 3, axis=1), jnp.concatenate(parts, axis=0),
                  preferred_element_type=F32)
    cum_end = cum[c_len - 1:c_len, :]
    w_in = jnp.exp(cum)
    w_inv = jnp.exp(-cum)
    w_prev = jnp.exp(cum - logw)
    w_tail = jnp.exp(cum_end - cum)
    w_end = jnp.exp(cum_end)

    al_t = -kk * w_prev
    be_t = kk * a * w_inv
    k_t = k * w_inv
    r_t = r * w_in
    be_e = kk * a * w_tail
    k_e = k * w_tail

    n2 = 2 * c_len
    ii = lax.broadcasted_iota(jnp.int32, (n2, n2), 0)
    jj = lax.broadcasted_iota(jnp.int32, (n2, n2), 1)
    strict = jj < ii
    incl = jj <= ii
    eye = jnp.where(ii == jj, 1.0, 0.0).astype(F32)

    ys = []
    for j in range(n_pairs):
        sl = slice(j * LANES, (j + 1) * LANES)
        al_s = _stack_heads(al_t[:, sl])
        r_s = _stack_heads(r_t[:, sl])
        be_s = _stack_heads(be_t[:, sl])
        k_s = _stack_heads(k_t[:, sl])
        v_s = _stack_heads(v[:, sl])
        bee_s = _stack_heads(be_e[:, sl])
        ke_s = _stack_heads(k_e[:, sl])
        state = state_ref[j]

        aa = _mm(jnp.concatenate([al_s, r_s], axis=0), jnp.concatenate([be_s, k_s], axis=0),
                 passes=3, nt=True)
        zero = jnp.zeros((n2, n2), F32)
        a_ab = jnp.where(strict, aa[:n2, :n2], zero)
        a_ak = jnp.where(strict, aa[:n2, n2:], zero)
        a_rb = jnp.where(incl, aa[n2:, :n2], zero)
        a_rk = jnp.where(incl, aa[n2:, n2:], zero)

        t_inv = eye + jnp.where((ii // 2 == jj // 2), a_ab, zero)
        blk = 4
        while blk <= c_len:
            half = blk // 2
            m = (ii // blk == jj // blk) & (ii % blk >= half) & (jj % blk < half)
            t_inv = t_inv + _mm(_mm(t_inv, jnp.where(m, a_ab, zero), passes=3), t_inv, passes=3)
            blk *= 2

        x = _mm(al_s, state, passes=3, nt=True) + _mm(a_ak, v_s, passes=3)
        u = _mm(t_inv, x, passes=3)
        y_s = (_mm(r_s, state, passes=3, nt=True) + _mm(a_rb, u, passes=3)
               + _mm(a_rk, v_s, passes=3))
        ys.append(y_s[:c_len] + y_s[c_len:])
        state_ref[j] = (state * w_end[:, sl] + _mm(u.T, bee_s, passes=3)
                        + _mm(v_s.T, ke_s, passes=3))

    y = jnp.concatenate(ys, axis=1)
    mean = _mm_exact_rhs(y, gones) * (1.0 / HEAD_DIM)
    yc = y - mean
    var = _mm_exact_rhs(yc * yc, gones) * (1.0 / HEAD_DIM)
    yn = yc * lax.rsqrt(var + LNX_EPS) * lng_ref[...] + lnb_ref[...]
    bonus = _mm_exact_rhs(r * k * rk_ref[...], gones) * v
    o_ref[0] = ((yn + bonus) * _silu(z_ref[0])).astype(o_ref.dtype)


def rwkv_call(p_a, z, mu, w0, w_up, a0, a_up, k_k, k_a, r_k, lnx_g, lnx_b):
    bsz, t_len, _ = p_a.shape
    assert t_len % RCHUNK == 0
    lora_w = jnp.zeros((2 * LORA, 2 * A_WIDTH), F32)
    lora_w = lora_w.at[:LORA, :A_WIDTH].set(w_up).at[LORA:, A_WIDTH:].set(a_up)
    row = lambda a: a.reshape(1, -1)
    const = lambda shape: pl.BlockSpec(shape, lambda b, i: (0,) * len(shape))
    return pl.pallas_call(
        _rwkv_kernel,
        out_shape=jax.ShapeDtypeStruct((bsz, t_len, A_WIDTH), BF16),
        grid=(bsz, t_len // RCHUNK),
        in_specs=[
            pl.BlockSpec((1, RCHUNK, SHIFT_W), lambda b, i: (b, i, 0)),
            pl.BlockSpec((1, RCHUNK, A_WIDTH), lambda b, i: (b, i, 0)),
            const((1, SHIFT_W)), const((1, A_WIDTH)), const((1, A_WIDTH)),
            const((2 * LORA, 2 * A_WIDTH)),
            const((1, A_WIDTH)), const((1, A_WIDTH)), const((1, A_WIDTH)),
            const((1, A_WIDTH)), const((1, A_WIDTH)),
            const((A_WIDTH, A_WIDTH)),
        ],
        out_specs=pl.BlockSpec((1, RCHUNK, A_WIDTH), lambda b, i: (b, i, 0)),
        scratch_shapes=[
            pltpu.VMEM((A_WIDTH // LANES, LANES, LANES), F32),
            pltpu.VMEM((8, SHIFT_W), F32),
        ],
        compiler_params=_cparams(("arbitrary", "arbitrary")),
        name="rwkv7",
    )(p_a, z, row(mu), row(w0), row(a0), lora_w, row(k_k), row(k_a), row(r_k),
      row(lnx_g), row(lnx_b), _group_ones(A_WIDTH))


def _gmlp_kernel(u_ref, v_ref, z_ref, lng_ref, lnb_ref, w_ref, bias_ref, gones_ref, o_ref):
    v = v_ref[0]
    gones = gones_ref[...]
    mean = _mm_exact_rhs(v, gones) * (1.0 / HEAD_DIM)
    vc = v - mean
    var = _mm_exact_rhs(vc * vc, gones) * (1.0 / HEAD_DIM)
    vn = vc * lax.rsqrt(var + GMLP_LN_EPS) * lng_ref[...] + lnb_ref[...]
    ti = lax.broadcasted_iota(jnp.int32, (CHUNK, CHUNK), 0)
    si = lax.broadcasted_iota(jnp.int32, (CHUNK, CHUNK), 1)
    causal = si <= ti
    mixed = []
    for j in range(B_WIDTH // LANES):
        w0 = jnp.where(causal, w_ref[2 * j], 0.0)
        w1 = jnp.where(causal, w_ref[2 * j + 1], 0.0)
        vn_s = _stack_heads(vn[:, j * LANES:(j + 1) * LANES])
        mixed.append(_mm(jnp.concatenate([w0, w1], axis=1), vn_s))
    mixed = jnp.concatenate(mixed, axis=1) + bias_ref[...]
    o_ref[0] = (u_ref[0] * mixed * _silu(z_ref[0])).astype(o_ref.dtype)


def gmlp_call(u, v, z, ln_g, ln_b, w_s, b_s):
    bsz, t_len, _ = u.shape
    groups = w_s.shape[0]
    bias = jnp.repeat(b_s.T, HEAD_DIM, axis=1)
    row = lambda a: a.reshape(1, -1)
    const = lambda shape: pl.BlockSpec(shape, lambda b, i: (0,) * len(shape))
    blk = pl.BlockSpec((1, CHUNK, B_WIDTH), lambda b, i: (b, i, 0))
    return pl.pallas_call(
        _gmlp_kernel,
        out_shape=jax.ShapeDtypeStruct((bsz, t_len, B_WIDTH), BF16),
        grid=(bsz, t_len // CHUNK),
        in_specs=[
            blk, blk,
            pl.BlockSpec((1, CHUNK, B_WIDTH), lambda b, i: (b, i, 1)),
            const((1, B_WIDTH)), const((1, B_WIDTH)),
            const((groups, CHUNK, CHUNK)), const((CHUNK, B_WIDTH)), const((B_WIDTH, B_WIDTH)),
        ],
        out_specs=blk,
        compiler_params=_cparams(("arbitrary", "arbitrary")),
        name="gmlp",
    )(u, v, z, row(ln_g), row(ln_b), w_s, bias, _group_ones(B_WIDTH))


def _swa_kernel(sink_ref, q_ref, kp_ref, kc_ref, vp_ref, vc_ref, z_ref, o_ref):
    n = pl.program_id(1)
    w = WINDOW
    group = C_HEADS // C_KV_HEADS
    kcat = jnp.concatenate([kp_ref[0], kc_ref[0]], axis=0)
    vcat = jnp.concatenate([vp_ref[0], vc_ref[0]], axis=0)
    first = lax.broadcasted_iota(jnp.int32, kcat.shape, 1) < HEAD_DIM
    kroll = pltpu.roll(kcat, HEAD_DIM, 1)
    vroll = pltpu.roll(vcat, HEAD_DIM, 1)

    qi = lax.broadcasted_iota(jnp.int32, (2 * w, 2 * w), 0) % w
    kj = lax.broadcasted_iota(jnp.int32, (2 * w, 2 * w), 1)
    dist = qi + w - kj
    valid = (dist >= 0) & (dist < w) & ((kj >= w) | (n > 0))
    distf = dist.astype(F32)
    upper = lax.broadcasted_iota(jnp.int32, (2 * w, 2 * w), 0) < w
    lane_first = lax.broadcasted_iota(jnp.int32, (w, LANES), 1) < HEAD_DIM
    slopes = [2.0 ** (-8.0 * (h + 1) / C_HEADS) for h in range(C_HEADS)]

    outs = []
    for kv in range(C_KV_HEADS):
        if kv == 0:
            kdup = jnp.where(first, kcat, kroll).astype(BF16)
            vdup = jnp.where(first, vcat, vroll).astype(BF16)
        else:
            kdup = jnp.where(first, kroll, kcat).astype(BF16)
            vdup = jnp.where(first, vroll, vcat).astype(BF16)
        for pi in range(group // 2):
            h0 = kv * group + 2 * pi
            q_s = _stack_heads(q_ref[0, :, h0 * HEAD_DIM:(h0 + 2) * HEAD_DIM]).astype(BF16)
            s = lax.dot_general(q_s, kdup, (((1,), (1,)), ((), ())),
                                preferred_element_type=F32) * (HEAD_DIM ** -0.5)
            slope = jnp.where(upper, slopes[h0], slopes[h0 + 1])
            sink = jnp.where(upper[:, 0:1], sink_ref[h0], sink_ref[h0 + 1])
            s = jnp.where(valid, s - slope * distf, -jnp.inf)
            m = jnp.maximum(jnp.max(s, axis=-1, keepdims=True), sink)
            e = jnp.exp(s - m)
            probs = e / (jnp.sum(e, axis=-1, keepdims=True) + jnp.exp(sink - m))
            o = jnp.dot(probs.astype(BF16), vdup, preferred_element_type=F32)
            outs.append(jnp.where(lane_first, o[:w], o[w:]))
    out = jnp.concatenate(outs, axis=1)
    o_ref[0] = (out * _silu(z_ref[0])).astype(o_ref.dtype)


def swa_call(q, k, v, z, sinks):
    bsz, t_len, mix = q.shape
    nb = t_len // WINDOW
    kv_w = k.shape[-1]
    cur = lambda b, i, *_: (b, i, 0)
    prev = lambda b, i, *_: (b, jnp.maximum(i - 1, 0), 0)
    grid_spec = pltpu.PrefetchScalarGridSpec(
        num_scalar_prefetch=1,
        grid=(bsz, nb),
        in_specs=[
            pl.BlockSpec((1, WINDOW, mix), cur),
            pl.BlockSpec((1, WINDOW, kv_w), prev), pl.BlockSpec((1, WINDOW, kv_w), cur),
            pl.BlockSpec((1, WINDOW, kv_w), prev), pl.BlockSpec((1, WINDOW, kv_w), cur),
            pl.BlockSpec((1, WINDOW, mix), cur),
        ],
        out_specs=pl.BlockSpec((1, WINDOW, mix), cur),
    )
    return pl.pallas_call(
        _swa_kernel,
        out_shape=jax.ShapeDtypeStruct((bsz, t_len, mix), BF16),
        grid_spec=grid_spec,
        compiler_params=_cparams(("arbitrary", "arbitrary")),
        name="swa",
    )(sinks, q, k, k, v, v, z)


def _out_proj_kernel(*refs, n_y, final):
    y_refs = refs[:n_y]
    w_ref, x_ref, gate_ref = refs[n_y:n_y + 3]
    rest = refs[n_y + 3:]
    acc = None
    off = 0
    for y_ref in y_refs:
        wd = y_ref.shape[-1]
        part = jnp.dot(y_ref[0], w_ref[off:off + wd, :], preferred_element_type=F32)
        acc = part if acc is None else acc + part
        off += wd
    x = x_ref[0] + gate_ref[0] * acc
    if final:
        g_ref, o_ref = rest
        r = lax.rsqrt(jnp.mean(x * x, axis=-1, keepdims=True) + NORM_EPS)
        o_ref[0] = (x * r) * g_ref[...]
    else:
        (o_ref,) = rest
        o_ref[0] = x


def out_proj_call(ys, w_bf16, x, gate, final_g=None, tm=512):
    bsz, t_len, d = x.shape
    final = final_g is not None
    tm = min(tm, t_len)
    assert t_len % tm == 0
    in_specs = [pl.BlockSpec((1, tm, y.shape[-1]), lambda b, i: (b, i, 0)) for y in ys]
    in_specs += [
        pl.BlockSpec(w_bf16.shape, lambda b, i: (0, 0)),
        pl.BlockSpec((1, tm, d), lambda b, i: (b, i, 0)),
        pl.BlockSpec((1, 1, d), lambda b, i: (b, 0, 0)),
    ]
    args = list(ys) + [w_bf16, x, gate]
    if final:
        in_specs.append(pl.BlockSpec((1, d), lambda b, i: (0, 0)))
        args.append(final_g.reshape(1, d))
    return pl.pallas_call(
        functools.partial(_out_proj_kernel, n_y=len(ys), final=final),
        out_shape=jax.ShapeDtypeStruct((bsz, t_len, d), F32),
        grid=(bsz, t_len // tm),
        in_specs=in_specs,
        out_specs=pl.BlockSpec((1, tm, d), lambda b, i: (b, i, 0)),
        compiler_params=_cparams(("arbitrary", "arbitrary")),
        name="out_proj_final" if final else "out_proj",
    )(*args)


def kernel(x, c, ada_w, ada_b, norm_g, e_w_in, e_mu, e_w0, e_w_up, e_a0, e_a_up, e_k_k, e_k_a, e_r_k,
           e_lnx_g, e_lnx_b, e_sg_ln_g, e_sg_ln_b, e_sg_w, e_sg_b, e_w_out, o_w_in, o_sinks, o_w_out,
           final_g):
    depth = ada_w.shape[0]
    d = x.shape[-1]
    mod = adaln_call(c, ada_w, ada_b)
    for i in range(depth):
        shift = mod[i, :, None, 0:d]
        scale = mod[i, :, None, d:2 * d]
        gate = mod[i, :, None, 2 * d:3 * d]
        j = i // 2
        last = i == depth - 1
        if i % 2 == 0:
            p_a, u, v, z = in_proj_call(x, norm_g[i], scale, shift, e_w_in[j].astype(BF16),
                                        (SHIFT_W, B_WIDTH, B_WIDTH, A_WIDTH + B_WIDTH))
            y_a = rwkv_call(p_a, z, e_mu[j], e_w0[j], e_w_up[j], e_a0[j], e_a_up[j], e_k_k[j], e_k_a[j],
                            e_r_k[j].reshape(-1), e_lnx_g[j], e_lnx_b[j])
            y_b = gmlp_call(u, v, z, e_sg_ln_g[j], e_sg_ln_b[j], e_sg_w[j], e_sg_b[j])
            ys, w_out = (y_a, y_b), e_w_out[j]
        else:
            mix = o_w_out.shape[1]
            kv_w = C_KV_HEADS * HEAD_DIM
            q, k, v, z = in_proj_call(x, norm_g[i], scale, shift, o_w_in[j].astype(BF16),
                                      (mix, kv_w, kv_w, mix))
            ys, w_out = (swa_call(q, k, v, z, o_sinks[j]),), o_w_out[j]
        x = out_proj_call(ys, w_out.astype(BF16), x, gate, final_g if last else None)
    if depth == 0:
        raise ValueError("depth must be positive")
    return x
```

```python
import functools

import jax
import jax.numpy as jnp
import numpy as np
from jax import lax
from jax.experimental import pallas as pl
from jax.experimental.pallas import tpu as pltpu

F32 = jnp.float32
BF16 = jnp.bfloat16

HEAD_DIM = 64
LANES = 128
NORM_EPS = 1e-6
LNX_EPS = 64e-5
GMLP_LN_EPS = 1e-5
A_WIDTH = 512
B_WIDTH = 512
LORA = 64
SHIFT_W = 3 * A_WIDTH + 2 * LORA
CHUNK = 128
WINDOW = 128
RCHUNK = 64
RBLOCK = 256
C_HEADS = 16
C_KV_HEADS = 2
VMEM_LIMIT = 56 * 1024 * 1024


def _cparams(sem):
    return pltpu.CompilerParams(dimension_semantics=sem, vmem_limit_bytes=VMEM_LIMIT)


def _split_bf16(a):
    hi = a.astype(BF16)
    lo = (a - hi.astype(F32)).astype(BF16)
    return hi, lo


def _mm(a, b, passes=1, nt=False):
    dn = (((1,), (1 if nt else 0,)), ((), ()))
    if passes == 1:
        return lax.dot_general(a.astype(BF16), b.astype(BF16), dn, preferred_element_type=F32)
    a_hi, a_lo = _split_bf16(a)
    b_hi, b_lo = _split_bf16(b)
    lhs = jnp.concatenate([a_hi, a_lo, a_hi], axis=1)
    rhs = jnp.concatenate([b_hi, b_hi, b_lo], axis=1 if nt else 0)
    return lax.dot_general(lhs, rhs, dn, preferred_element_type=F32)


def _bf16_pieces(a, pieces):
    parts = []
    rem = a
    for _ in range(pieces):
        p = rem.astype(BF16)
        parts.append(p)
        rem = rem - p.astype(F32)
    return parts


def _mm_exact_rhs(a, b_bf16, pieces=2):
    lhs = jnp.concatenate(_bf16_pieces(a, pieces), axis=1)
    rhs = jnp.concatenate([b_bf16] * pieces, axis=0)
    return jnp.dot(lhs, rhs, preferred_element_type=F32)


def _silu(x):
    return x / (1.0 + jnp.exp(-x))


def _group_ones(width):
    g = np.arange(width) // HEAD_DIM
    return jnp.asarray((g[:, None] == g[None, :]).astype(np.float32), BF16)


def _adaln_kernel(c_ref, w_ref, b_ref, o_ref):
    cond = _silu(c_ref[...])
    o_ref[0] = _mm(cond, w_ref[0], passes=3) + b_ref[0]


def adaln_call(c, ada_w, ada_b):
    depth, d, d3 = ada_w.shape
    bsz = c.shape[0]
    tn = d
    return pl.pallas_call(
        _adaln_kernel,
        out_shape=jax.ShapeDtypeStruct((depth, bsz, d3), F32),
        grid=(depth, d3 // tn),
        in_specs=[
            pl.BlockSpec((bsz, d), lambda i, j: (0, 0)),
            pl.BlockSpec((1, d, tn), lambda i, j: (i, 0, j)),
            pl.BlockSpec((1, 1, tn), lambda i, j: (i, 0, j)),
        ],
        out_specs=pl.BlockSpec((1, bsz, tn), lambda i, j: (i, 0, j)),
        compiler_params=_cparams(("arbitrary", "arbitrary")),
        name="adaln",
    )(c, ada_w, ada_b.reshape(depth, 1, d3))


def _modulated_norm(x, g, scale, shift):
    r = lax.rsqrt(jnp.mean(x * x, axis=-1, keepdims=True) + NORM_EPS)
    return (x * r) * g * (1.0 + scale) + shift


def _in_proj_kernel(x_ref, g_ref, scale_ref, shift_ref, w_ref, *o_refs, widths):
    h = _modulated_norm(x_ref[0], g_ref[...], scale_ref[0], shift_ref[0]).astype(BF16)
    off = 0
    for o_ref, wd in zip(o_refs, widths):
        o_ref[0] = jnp.dot(h, w_ref[:, off:off + wd], preferred_element_type=F32).astype(o_ref.dtype)
        off += wd


def in_proj_call(x, g, scale, shift, w_bf16, widths, tm=512):
    bsz, t_len, d = x.shape
    n_out = w_bf16.shape[1]
    tm = min(tm, t_len)
    assert sum(widths) == n_out and t_len % tm == 0
    vec = pl.BlockSpec((1, 1, d), lambda b, i: (b, 0, 0))
    return pl.pallas_call(
        functools.partial(_in_proj_kernel, widths=widths),
        out_shape=[jax.ShapeDtypeStruct((bsz, t_len, wd), F32) for wd in widths],
        grid=(bsz, t_len // tm),
        in_specs=[
            pl.BlockSpec((1, tm, d), lambda b, i: (b, i, 0)),
            pl.BlockSpec((1, d), lambda b, i: (0, 0)),
            vec, vec,
            pl.BlockSpec((d, n_out), lambda b, i: (0, 0)),
        ],
        out_specs=[pl.BlockSpec((1, tm, wd), lambda b, i: (b, i, 0)) for wd in widths],
        compiler_params=_cparams(("arbitrary", "arbitrary")),
        name="in_proj",
    )(x, g.reshape(1, d), scale, shift, w_bf16)


def _stack_heads(z):
    first = lax.broadcasted_iota(jnp.int32, z.shape, 1) < HEAD_DIM
    zero = jnp.zeros_like(z)
    return jnp.concatenate([jnp.where(first, z, zero), jnp.where(first, zero, z)], axis=0)


def _dot(a, b):
    return jnp.dot(a, b, preferred_element_type=F32)


def _dot_nt(a, b):
    return lax.dot_general(a, b, (((1,), (1,)), ((), ())), preferred_element_type=F32)


def _rwkv_chunk_terms(insts, masks):
    strict_b, incl_b, eye, level_masks = masks
    n2 = insts[0][0].shape[0]
    aas = [_dot_nt(jnp.concatenate([al_s, r_s], axis=0), jnp.concatenate([be_s, k_s], axis=0))
           for al_s, r_s, be_s, k_s, _, _, _ in insts]
    a_ab = [aa[:n2, :n2].astype(BF16) * strict_b for aa in aas]
    a_ak = [aa[:n2, n2:].astype(BF16) * strict_b for aa in aas]
    a_rb = [aa[n2:, :n2].astype(BF16) * incl_b for aa in aas]
    a_rk = [aa[n2:, n2:].astype(BF16) * incl_b for aa in aas]

    t_inv = [eye + (a * level_masks[0]).astype(F32) for a in a_ab]
    for m in level_masks[1:]:
        t_b = [t.astype(BF16) for t in t_inv]
        left = [_dot(tb, a * m).astype(BF16) for tb, a in zip(t_b, a_ab)]
        t_inv = [t + _dot(lf, tb) for t, lf, tb in zip(t_inv, left, t_b)]
    t_b = [t.astype(BF16) for t in t_inv]

    akv = [_dot(a, inst[4]).astype(BF16) for a, inst in zip(a_ak, insts)]
    tz = [_dot(tb, jnp.concatenate([inst[0], x], axis=1))
          for tb, inst, x in zip(t_b, insts, akv)]
    tz_b = [t.astype(BF16) for t in tz]
    rz = [_dot(a, t) for a, t in zip(a_rb, tz_b)]
    ykv = [_dot(a, inst[4]) for a, inst in zip(a_rk, insts)]
    zz = [_dot(t.T.astype(BF16), inst[5]) for t, inst in zip(tz, insts)]
    vke = [_dot(inst[4].astype(F32).T.astype(BF16), inst[6]) for inst in insts]
    out = []
    for i, inst in enumerate(insts):
        rq = (inst[1].astype(F32) + rz[i][:, :n2]).astype(BF16)
        yq = rz[i][:, n2:] + ykv[i]
        out.append((rq, yq, zz[i][:n2].astype(BF16), zz[i][n2:] + vke[i]))
    return out


def _rwkv_kernel(p_ref, z_ref, mu_ref, w0_ref, a0_ref, lora_ref, kk_ref, ka_ref, rk_ref,
                 lng_ref, lnb_ref, gones_ref, o_ref, state_ref, prev_ref):
    c_len = RCHUNK
    t_blk = p_ref.shape[1]
    n_chunks = t_blk // c_len
    n_pairs = A_WIDTH // LANES

    @pl.when(pl.program_id(1) == 0)
    def _():
        state_ref[...] = jnp.zeros_like(state_ref)
        prev_ref[...] = jnp.zeros_like(prev_ref)

    p = p_ref[0]
    row = lax.broadcasted_iota(jnp.int32, p.shape, 0)
    shifted = jnp.where(row == 0, prev_ref[0:1, :], pltpu.roll(p, 1, 0))
    prev_ref[0:1, :] = p[t_blk - 1:t_blk, :]
    ps = p + mu_ref[...] * (shifted - p)

    r = ps[:, 0:A_WIDTH]
    k = ps[:, A_WIDTH:2 * A_WIDTH]
    v = ps[:, 2 * A_WIDTH:3 * A_WIDTH]
    lo = ps[:, 3 * A_WIDTH:SHIFT_W]
    lane = lax.broadcasted_iota(jnp.int32, lo.shape, 1)
    lo = jnp.where(lane < LORA, jnp.tanh(lo), lo)
    up = _mm(lo, lora_ref[...], passes=3)
    xw = -(w0_ref[...] + up[:, :A_WIDTH])
    softplus = jnp.maximum(xw, 0.0) + jnp.log(1.0 + jnp.exp(-jnp.abs(xw)))
    logw = -jnp.exp(-softplus - 0.5)
    a = 1.0 / (1.0 + jnp.exp(-(a0_ref[...] + up[:, A_WIDTH:])))

    gones = gones_ref[...]
    kk = k * kk_ref[...]
    kk = kk / jnp.maximum(jnp.sqrt(_mm_exact_rhs(kk * kk, gones)), 1e-12)
    k = k * (1.0 + (a - 1.0) * ka_ref[...])

    ti = lax.broadcasted_iota(jnp.int32, (2 * c_len, c_len), 0)
    si = lax.broadcasted_iota(jnp.int32, (2 * c_len, c_len), 1)
    tri = jnp.where((si <= ti) | (ti >= c_len), 1.0, 0.0).astype(BF16)
    tri3 = jnp.concatenate([tri] * 3, axis=1)
    cums, totals = [], []
    for ci in range(n_chunks):
        parts = _bf16_pieces(logw[ci * c_len:(ci + 1) * c_len], 3)
        cc = jnp.dot(tri3, jnp.concatenate(parts, axis=0), preferred_element_type=F32)
        cums.append(cc[:c_len])
        totals.append(cc[c_len:])
    cum = jnp.concatenate(cums, axis=0)
    total = jnp.concatenate(totals, axis=0)
    w_inv = jnp.exp(-cum)
    w_tail = jnp.exp(total - cum)
    w_end = jnp.exp(total)
    beta = kk * a
    al_t = (-kk * jnp.exp(cum - logw)).astype(BF16)
    r_t = (r * jnp.exp(cum)).astype(BF16)
    be_t = (beta * w_inv).astype(BF16)
    k_t = (k * w_inv).astype(BF16)
    be_e = (beta * w_tail).astype(BF16)
    k_e = (k * w_tail).astype(BF16)
    v_b = v.astype(BF16)

    n2 = 2 * c_len
    ii = lax.broadcasted_iota(jnp.int32, (n2, n2), 0)
    jj = lax.broadcasted_iota(jnp.int32, (n2, n2), 1)
    as_mask = lambda cond: jnp.where(cond, 1.0, 0.0).astype(BF16)
    strict_b = as_mask(jj < ii)
    incl_b = as_mask(jj <= ii)
    eye = jnp.where(ii == jj, 1.0, 0.0).astype(F32)
    level_masks = []
    blk = 2
    while blk <= c_len:
        half = blk // 2
        m = (ii // blk == jj // blk) & (ii % blk >= half) & (jj % blk < half)
        level_masks.append(as_mask(m))
        blk *= 2
    masks = (strict_b, incl_b, eye, level_masks)

    keys, insts = [], []
    for ci in range(n_chunks):
        rows = slice(ci * c_len, (ci + 1) * c_len)
        for j in range(n_pairs):
            sl = slice(j * LANES, (j + 1) * LANES)
            keys.append((ci, j))
            insts.append(tuple(_stack_heads(t[rows, sl]) for t in (al_t, r_t, be_t, k_t, v_b, be_e, k_e)))
    terms = dict(zip(keys, _rwkv_chunk_terms(insts, masks)))

    y_rows = []
    states = [state_ref[j] for j in range(n_pairs)]
    for ci in range(n_chunks):
        ys = []
        for j in range(n_pairs):
            rq, yq, pmat, qmat = terms[ci, j]
            s_b = states[j].astype(BF16)
            y_s = _dot_nt(rq, s_b) + yq
            ys.append(y_s[:c_len] + y_s[c_len:])
            w_row = w_end[ci * c_len:ci * c_len + 1, j * LANES:(j + 1) * LANES]
            states[j] = states[j] * w_row + _dot(s_b, pmat) + qmat
        y_rows.append(jnp.concatenate(ys, axis=1))
    for j in range(n_pairs):
        state_ref[j] = states[j]

    y = jnp.concatenate(y_rows, axis=0)
    mean = _mm_exact_rhs(y, gones) * (1.0 / HEAD_DIM)
    yc = y - mean
    var = _mm_exact_rhs(yc * yc, gones) * (1.0 / HEAD_DIM)
    yn = yc * lax.rsqrt(var + LNX_EPS) * lng_ref[...] + lnb_ref[...]
    bonus = _mm_exact_rhs(r * k * rk_ref[...], gones) * v
    o_ref[0] = ((yn + bonus) * _silu(z_ref[0])).astype(o_ref.dtype)


def rwkv_call(p_a, z, mu, w0, w_up, a0, a_up, k_k, k_a, r_k, lnx_g, lnx_b):
    bsz, t_len, _ = p_a.shape
    t_blk = min(RBLOCK, t_len)
    assert t_len % t_blk == 0 and t_blk % RCHUNK == 0
    lora_w = jnp.zeros((2 * LORA, 2 * A_WIDTH), F32)
    lora_w = lora_w.at[:LORA, :A_WIDTH].set(w_up).at[LORA:, A_WIDTH:].set(a_up)
    row = lambda a: a.reshape(1, -1)
    const = lambda shape: pl.BlockSpec(shape, lambda b, i: (0,) * len(shape))
    return pl.pallas_call(
        _rwkv_kernel,
        out_shape=jax.ShapeDtypeStruct((bsz, t_len, A_WIDTH), BF16),
        grid=(bsz, t_len // t_blk),
        in_specs=[
            pl.BlockSpec((1, t_blk, SHIFT_W), lambda b, i: (b, i, 0)),
            pl.BlockSpec((1, t_blk, A_WIDTH), lambda b, i: (b, i, 0)),
            const((1, SHIFT_W)), const((1, A_WIDTH)), const((1, A_WIDTH)),
            const((2 * LORA, 2 * A_WIDTH)),
            const((1, A_WIDTH)), const((1, A_WIDTH)), const((1, A_WIDTH)),
            const((1, A_WIDTH)), const((1, A_WIDTH)),
            const((A_WIDTH, A_WIDTH)),
        ],
        out_specs=pl.BlockSpec((1, t_blk, A_WIDTH), lambda b, i: (b, i, 0)),
        scratch_shapes=[
            pltpu.VMEM((A_WIDTH // LANES, LANES, LANES), F32),
            pltpu.VMEM((8, SHIFT_W), F32),
        ],
        compiler_params=_cparams(("arbitrary", "arbitrary")),
        name="rwkv7",
    )(p_a, z, row(mu), row(w0), row(a0), lora_w, row(k_k), row(k_a), row(r_k),
      row(lnx_g), row(lnx_b), _group_ones(A_WIDTH))


def _gmlp_kernel(u_ref, v_ref, z_ref, lng_ref, lnb_ref, w_ref, bias_ref, gones_ref, o_ref):
    v = v_ref[0]
    gones = gones_ref[...]
    mean = _mm_exact_rhs(v, gones) * (1.0 / HEAD_DIM)
    vc = v - mean
    var = _mm_exact_rhs(vc * vc, gones) * (1.0 / HEAD_DIM)
    vn = vc * lax.rsqrt(var + GMLP_LN_EPS) * lng_ref[...] + lnb_ref[...]
    ti = lax.broadcasted_iota(jnp.int32, (CHUNK, CHUNK), 0)
    si = lax.broadcasted_iota(jnp.int32, (CHUNK, CHUNK), 1)
    causal = si <= ti
    mixed = []
    for j in range(B_WIDTH // LANES):
        w0 = jnp.where(causal, w_ref[2 * j], 0.0)
        w1 = jnp.where(causal, w_ref[2 * j + 1], 0.0)
        vn_s = _stack_heads(vn[:, j * LANES:(j + 1) * LANES])
        mixed.append(_mm(jnp.concatenate([w0, w1], axis=1), vn_s))
    mixed = jnp.concatenate(mixed, axis=1) + bias_ref[...]
    o_ref[0] = (u_ref[0] * mixed * _silu(z_ref[0])).astype(o_ref.dtype)


def gmlp_call(u, v, z, ln_g, ln_b, w_s, b_s):
    bsz, t_len, _ = u.shape
    groups = w_s.shape[0]
    bias = jnp.repeat(b_s.T, HEAD_DIM, axis=1)
    row = lambda a: a.reshape(1, -1)
    const = lambda shape: pl.BlockSpec(shape, lambda b, i: (0,) * len(shape))
    blk = pl.BlockSpec((1, CHUNK, B_WIDTH), lambda b, i: (b, i, 0))
    return pl.pallas_call(
        _gmlp_kernel,
        out_shape=jax.ShapeDtypeStruct((bsz, t_len, B_WIDTH), BF16),
        grid=(bsz, t_len // CHUNK),
        in_specs=[
            blk, blk,
            pl.BlockSpec((1, CHUNK, B_WIDTH), lambda b, i: (b, i, 1)),
            const((1, B_WIDTH)), const((1, B_WIDTH)),
            const((groups, CHUNK, CHUNK)), const((CHUNK, B_WIDTH)), const((B_WIDTH, B_WIDTH)),
        ],
        out_specs=blk,
        compiler_params=_cparams(("arbitrary", "arbitrary")),
        name="gmlp",
    )(u, v, z, row(ln_g), row(ln_b), w_s, bias, _group_ones(B_WIDTH))


def _swa_kernel(sink_ref, q_ref, kp_ref, kc_ref, vp_ref, vc_ref, z_ref, o_ref):
    n = pl.program_id(1)
    w = WINDOW
    group = C_HEADS // C_KV_HEADS
    kcat = jnp.concatenate([kp_ref[0], kc_ref[0]], axis=0)
    vcat = jnp.concatenate([vp_ref[0], vc_ref[0]], axis=0)
    first = lax.broadcasted_iota(jnp.int32, kcat.shape, 1) < HEAD_DIM
    kroll = pltpu.roll(kcat, HEAD_DIM, 1)
    vroll = pltpu.roll(vcat, HEAD_DIM, 1)

    qi = lax.broadcasted_iota(jnp.int32, (2 * w, 2 * w), 0) % w
    kj = lax.broadcasted_iota(jnp.int32, (2 * w, 2 * w), 1)
    dist = qi + w - kj
    valid = (dist >= 0) & (dist < w) & ((kj >= w) | (n > 0))
    distf = dist.astype(F32)
    upper = lax.broadcasted_iota(jnp.int32, (2 * w, 2 * w), 0) < w
    lane_first = lax.broadcasted_iota(jnp.int32, (w, LANES), 1) < HEAD_DIM
    slopes = [2.0 ** (-8.0 * (h + 1) / C_HEADS) for h in range(C_HEADS)]

    outs = []
    for kv in range(C_KV_HEADS):
        if kv == 0:
            kdup = jnp.where(first, kcat, kroll).astype(BF16)
            vdup = jnp.where(first, vcat, vroll).astype(BF16)
        else:
            kdup = jnp.where(first, kroll, kcat).astype(BF16)
            vdup = jnp.where(first, vroll, vcat).astype(BF16)
        for pi in range(group // 2):
            h0 = kv * group + 2 * pi
            q_s = _stack_heads(q_ref[0, :, h0 * HEAD_DIM:(h0 + 2) * HEAD_DIM]).astype(BF16)
            s = lax.dot_general(q_s, kdup, (((1,), (1,)), ((), ())),
                                preferred_element_type=F32) * (HEAD_DIM ** -0.5)
            slope = jnp.where(upper, slopes[h0], slopes[h0 + 1])
            sink = jnp.where(upper[:, 0:1], sink_ref[h0], sink_ref[h0 + 1])
            s = jnp.where(valid, s - slope * distf, -jnp.inf)
            m = jnp.maximum(jnp.max(s, axis=-1, keepdims=True), sink)
            e = jnp.exp(s - m)
            probs = e / (jnp.sum(e, axis=-1, keepdims=True) + jnp.exp(sink - m))
            o = jnp.dot(probs.astype(BF16), vdup, preferred_element_type=F32)
            outs.append(jnp.where(lane_first, o[:w], o[w:]))
    out = jnp.concatenate(outs, axis=1)
    o_ref[0] = (out * _silu(z_ref[0])).astype(o_ref.dtype)


def swa_call(q, k, v, z, sinks):
    bsz, t_len, mix = q.shape
    nb = t_len // WINDOW
    kv_w = k.shape[-1]
    cur = lambda b, i, *_: (b, i, 0)
    prev = lambda b, i, *_: (b, jnp.maximum(i - 1, 0), 0)
    grid_spec = pltpu.PrefetchScalarGridSpec(
        num_scalar_prefetch=1,
        grid=(bsz, nb),
        in_specs=[
            pl.BlockSpec((1, WINDOW, mix), cur),
            pl.BlockSpec((1, WINDOW, kv_w), prev), pl.BlockSpec((1, WINDOW, kv_w), cur),
            pl.BlockSpec((1, WINDOW, kv_w), prev), pl.BlockSpec((1, WINDOW, kv_w), cur),
            pl.BlockSpec((1, WINDOW, mix), cur),
        ],
        out_specs=pl.BlockSpec((1, WINDOW, mix), cur),
    )
    return pl.pallas_call(
        _swa_kernel,
        out_shape=jax.ShapeDtypeStruct((bsz, t_len, mix), BF16),
        grid_spec=grid_spec,
        compiler_params=_cparams(("arbitrary", "arbitrary")),
        name="swa",
    )(sinks, q, k, k, v, v, z)


def _out_proj_kernel(*refs, n_y, final):
    y_refs = refs[:n_y]
    w_ref, x_ref, gate_ref = refs[n_y:n_y + 3]
    rest = refs[n_y + 3:]
    acc = None
    off = 0
    for y_ref in y_refs:
        wd = y_ref.shape[-1]
        part = jnp.dot(y_ref[0], w_ref[off:off + wd, :], preferred_element_type=F32)
        acc = part if acc is None else acc + part
        off += wd
    x = x_ref[0] + gate_ref[0] * acc
    if final:
        g_ref, o_ref = rest
        r = lax.rsqrt(jnp.mean(x * x, axis=-1, keepdims=True) + NORM_EPS)
        o_ref[0] = (x * r) * g_ref[...]
    else:
        (o_ref,) = rest
        o_ref[0] = x


def out_proj_call(ys, w_bf16, x, gate, final_g=None, tm=512):
    bsz, t_len, d = x.shape
    final = final_g is not None
    tm = min(tm, t_len)
    assert t_len % tm == 0
    in_specs = [pl.BlockSpec((1, tm, y.shape[-1]), lambda b, i: (b, i, 0)) for y in ys]
    in_specs += [
        pl.BlockSpec(w_bf16.shape, lambda b, i: (0, 0)),
        pl.BlockSpec((1, tm, d), lambda b, i: (b, i, 0)),
        pl.BlockSpec((1, 1, d), lambda b, i: (b, 0, 0)),
    ]
    args = list(ys) + [w_bf16, x, gate]
    if final:
        in_specs.append(pl.BlockSpec((1, d), lambda b, i: (0, 0)))
        args.append(final_g.reshape(1, d))
    return pl.pallas_call(
        functools.partial(_out_proj_kernel, n_y=len(ys), final=final),
        out_shape=jax.ShapeDtypeStruct((bsz, t_len, d), F32),
        grid=(bsz, t_len // tm),
        in_specs=in_specs,
        out_specs=pl.BlockSpec((1, tm, d), lambda b, i: (b, i, 0)),
        compiler_params=_cparams(("arbitrary", "arbitrary")),
        name="out_proj_final" if final else "out_proj",
    )(*args)


def kernel(x, c, ada_w, ada_b, norm_g, e_w_in, e_mu, e_w0, e_w_up, e_a0, e_a_up, e_k_k, e_k_a, e_r_k,
           e_lnx_g, e_lnx_b, e_sg_ln_g, e_sg_ln_b, e_sg_w, e_sg_b, e_w_out, o_w_in, o_sinks, o_w_out,
           final_g):
    depth = ada_w.shape[0]
    d = x.shape[-1]
    mod = adaln_call(c, ada_w, ada_b)
    for i in range(depth):
        shift = mod[i, :, None, 0:d]
        scale = mod[i, :, None, d:2 * d]
        gate = mod[i, :, None, 2 * d:3 * d]
        j = i // 2
        last = i == depth - 1
        if i % 2 == 0:
            p_a, u, v, z = in_proj_call(x, norm_g[i], scale, shift, e_w_in[j].astype(BF16),
                                        (SHIFT_W, B_WIDTH, B_WIDTH, A_WIDTH + B_WIDTH))
            y_a = rwkv_call(p_a, z, e_mu[j], e_w0[j], e_w_up[j], e_a0[j], e_a_up[j], e_k_k[j], e_k_a[j],
                            e_r_k[j].reshape(-1), e_lnx_g[j], e_lnx_b[j])
            y_b = gmlp_call(u, v, z, e_sg_ln_g[j], e_sg_ln_b[j], e_sg_w[j], e_sg_b[j])
            ys, w_out = (y_a, y_b), e_w_out[j]
        else:
            mix = o_w_out.shape[1]
            kv_w = C_KV_HEADS * HEAD_DIM
            q, k, v, z = in_proj_call(x, norm_g[i], scale, shift, o_w_in[j].astype(BF16),
                                      (mix, kv_w, kv_w, mix))
            ys, w_out = (swa_call(q, k, v, z, o_sinks[j]),), o_w_out[j]
        x = out_proj_call(ys, w_out.astype(BF16), x, gate, final_g if last else None)
    return x
```

```python
import functools

import jax
import jax.numpy as jnp
import numpy as np
from jax import lax
from jax.experimental import pallas as pl
from jax.experimental.pallas import tpu as pltpu

F32 = jnp.float32
BF16 = jnp.bfloat16

LOG2E = 1.4426950408889634
HEAD_DIM = 64
LANES = 128
NORM_EPS = 1e-6
LNX_EPS = 64e-5
GMLP_LN_EPS = 1e-5
A_WIDTH = 512
B_WIDTH = 512
LORA = 64
SHIFT_W = 3 * A_WIDTH + 2 * LORA
CHUNK = 128
WINDOW = 128
RCHUNK = 64
RBLOCK = 256
GBLOCK = 512
QBLOCK = 256
C_HEADS = 16
C_KV_HEADS = 2
VMEM_LIMIT = 56 * 1024 * 1024


def _cparams(sem):
    return pltpu.CompilerParams(dimension_semantics=sem, vmem_limit_bytes=VMEM_LIMIT)


def _split_bf16(a):
    hi = a.astype(BF16)
    lo = (a - hi.astype(F32)).astype(BF16)
    return hi, lo


def _mm(a, b, passes=1, nt=False):
    dn = (((1,), (1 if nt else 0,)), ((), ()))
    if passes == 1:
        return lax.dot_general(a.astype(BF16), b.astype(BF16), dn, preferred_element_type=F32)
    a_hi, a_lo = _split_bf16(a)
    b_hi, b_lo = _split_bf16(b)
    lhs = jnp.concatenate([a_hi, a_lo, a_hi], axis=1)
    rhs = jnp.concatenate([b_hi, b_hi, b_lo], axis=1 if nt else 0)
    return lax.dot_general(lhs, rhs, dn, preferred_element_type=F32)


def _bf16_pieces(a, pieces):
    parts = []
    rem = a
    for _ in range(pieces):
        p = rem.astype(BF16)
        parts.append(p)
        rem = rem - p.astype(F32)
    return parts


def _group_sum(a, pieces=2):
    ri = lax.broadcasted_iota(jnp.int32, (LANES, LANES), 0) // HEAD_DIM
    ci = lax.broadcasted_iota(jnp.int32, (LANES, LANES), 1) // HEAD_DIM
    ones = jnp.where(ri == ci, 1.0, 0.0).astype(BF16)
    rhs = jnp.concatenate([ones] * pieces, axis=0)
    parts = _bf16_pieces(a, pieces)
    outs = []
    for j in range(a.shape[1] // LANES):
        sl = slice(j * LANES, (j + 1) * LANES)
        lhs = jnp.concatenate([p[:, sl] for p in parts], axis=1)
        outs.append(jnp.dot(lhs, rhs, preferred_element_type=F32))
    return jnp.concatenate(outs, axis=1)


def _silu(x):
    return x / (1.0 + jnp.exp(-x))


def _adaln_kernel(c_ref, w_ref, b_ref, o_ref):
    cond = _silu(c_ref[...])
    o_ref[0] = _mm(cond, w_ref[0], passes=3) + b_ref[0]


def adaln_call(c, ada_w, ada_b):
    depth, d, d3 = ada_w.shape
    bsz = c.shape[0]
    tn = d
    return pl.pallas_call(
        _adaln_kernel,
        out_shape=jax.ShapeDtypeStruct((depth, bsz, d3), F32),
        grid=(depth, d3 // tn),
        in_specs=[
            pl.BlockSpec((bsz, d), lambda i, j: (0, 0)),
            pl.BlockSpec((1, d, tn), lambda i, j: (i, 0, j)),
            pl.BlockSpec((1, 1, tn), lambda i, j: (i, 0, j)),
        ],
        out_specs=pl.BlockSpec((1, bsz, tn), lambda i, j: (i, 0, j)),
        compiler_params=_cparams(("arbitrary", "arbitrary")),
        name="adaln",
    )(c, ada_w, ada_b.reshape(depth, 1, d3))


def _modulated_norm(x, g, scale, shift):
    r = lax.rsqrt(jnp.mean(x * x, axis=-1, keepdims=True) + NORM_EPS)
    return (x * r) * g * (1.0 + scale) + shift


def _in_proj_kernel(x_ref, g_ref, scale_ref, shift_ref, w_ref, *o_refs, widths):
    h = _modulated_norm(x_ref[0], g_ref[...], scale_ref[0], shift_ref[0]).astype(BF16)
    off = 0
    for o_ref, wd in zip(o_refs, widths):
        o_ref[0] = jnp.dot(h, w_ref[:, off:off + wd], preferred_element_type=F32).astype(o_ref.dtype)
        off += wd


def in_proj_call(x, g, scale, shift, w_bf16, widths, tm=512):
    bsz, t_len, d = x.shape
    n_out = w_bf16.shape[1]
    tm = min(tm, t_len)
    assert sum(widths) == n_out and t_len % tm == 0
    vec = pl.BlockSpec((1, 1, d), lambda b, i: (b, 0, 0))
    return pl.pallas_call(
        functools.partial(_in_proj_kernel, widths=widths),
        out_shape=[jax.ShapeDtypeStruct((bsz, t_len, wd), F32) for wd in widths],
        grid=(bsz, t_len // tm),
        in_specs=[
            pl.BlockSpec((1, tm, d), lambda b, i: (b, i, 0)),
            pl.BlockSpec((1, d), lambda b, i: (0, 0)),
            vec, vec,
            pl.BlockSpec((d, n_out), lambda b, i: (0, 0)),
        ],
        out_specs=[pl.BlockSpec((1, tm, wd), lambda b, i: (b, i, 0)) for wd in widths],
        compiler_params=_cparams(("arbitrary", "arbitrary")),
        name="in_proj",
    )(x, g.reshape(1, d), scale, shift, w_bf16)


def _stack_heads(z):
    first = lax.broadcasted_iota(jnp.int32, z.shape, 1) < HEAD_DIM
    zero = jnp.zeros_like(z)
    return jnp.concatenate([jnp.where(first, z, zero), jnp.where(first, zero, z)], axis=0)


def _dot(a, b):
    return jnp.dot(a, b, preferred_element_type=F32)


def _dot_nt(a, b):
    return lax.dot_general(a, b, (((1,), (1,)), ((), ())), preferred_element_type=F32)


def _rwkv_chunk_terms(insts, masks):
    strict_b, incl_b, eye, level_masks = masks
    n2 = insts[0][0].shape[0]
    aas = [_dot_nt(jnp.concatenate([al_s, r_s], axis=0), jnp.concatenate([be_s, k_s], axis=0))
           for al_s, r_s, be_s, k_s, _, _, _ in insts]
    a_ab = [aa[:n2, :n2].astype(BF16) * strict_b for aa in aas]
    a_ak = [aa[:n2, n2:].astype(BF16) * strict_b for aa in aas]
    a_rb = [aa[n2:, :n2].astype(BF16) * incl_b for aa in aas]
    a_rk = [aa[n2:, n2:].astype(BF16) * incl_b for aa in aas]

    t_inv = [eye + (a * level_masks[0]).astype(F32) for a in a_ab]
    for m in level_masks[1:]:
        t_b = [t.astype(BF16) for t in t_inv]
        left = [_dot(tb, a * m).astype(BF16) for tb, a in zip(t_b, a_ab)]
        t_inv = [t + _dot(lf, tb) for t, lf, tb in zip(t_inv, left, t_b)]
    t_b = [t.astype(BF16) for t in t_inv]

    akv = [_dot(a, inst[4]).astype(BF16) for a, inst in zip(a_ak, insts)]
    tz = [_dot(tb, jnp.concatenate([inst[0], x], axis=1))
          for tb, inst, x in zip(t_b, insts, akv)]
    tz_b = [t.astype(BF16) for t in tz]
    rz = [_dot(a, t) for a, t in zip(a_rb, tz_b)]
    ykv = [_dot(a, inst[4]) for a, inst in zip(a_rk, insts)]
    zz = [_dot(t.T.astype(BF16), inst[5]) for t, inst in zip(tz, insts)]
    vke = [_dot(inst[4].astype(F32).T.astype(BF16), inst[6]) for inst in insts]
    out = []
    for i, inst in enumerate(insts):
        rq = (inst[1].astype(F32) + rz[i][:, :n2]).astype(BF16)
        yq = rz[i][:, n2:] + ykv[i]
        out.append((rq, yq, zz[i][:n2].astype(BF16), zz[i][n2:] + vke[i]))
    return out


def _rwkv_kernel(p_ref, z_ref, mu_ref, w0_ref, a0_ref, lora_ref, kk_ref, ka_ref, rk_ref,
                 lng_ref, lnb_ref, o_ref, state_ref, prev_ref):
    c_len = RCHUNK
    t_blk = p_ref.shape[1]
    n_chunks = t_blk // c_len
    n_pairs = A_WIDTH // LANES

    @pl.when(pl.program_id(1) == 0)
    def _():
        state_ref[...] = jnp.zeros_like(state_ref)
        prev_ref[...] = jnp.zeros_like(prev_ref)

    p = p_ref[0]
    row = lax.broadcasted_iota(jnp.int32, p.shape, 0)
    shifted = jnp.where(row == 0, prev_ref[0:1, :], pltpu.roll(p, 1, 0))
    prev_ref[0:1, :] = p[t_blk - 1:t_blk, :]
    ps = p + mu_ref[...] * (shifted - p)

    r = ps[:, 0:A_WIDTH]
    k = ps[:, A_WIDTH:2 * A_WIDTH]
    v = ps[:, 2 * A_WIDTH:3 * A_WIDTH]
    lo = ps[:, 3 * A_WIDTH:SHIFT_W]
    lane = lax.broadcasted_iota(jnp.int32, lo.shape, 1)
    lo = jnp.where(lane < LORA, jnp.tanh(lo), lo)
    up = _mm(lo, lora_ref[...])
    xw = -(w0_ref[...] + up[:, :A_WIDTH])
    softplus = jnp.maximum(xw, 0.0) + jnp.log(1.0 + jnp.exp(-jnp.abs(xw)))
    logw = -jnp.exp(-softplus - 0.5)
    a = 1.0 / (1.0 + jnp.exp(-(a0_ref[...] + up[:, A_WIDTH:])))

    kk = k * kk_ref[...]
    kk = kk / jnp.maximum(jnp.sqrt(_group_sum(kk * kk)), 1e-12)
    k = k * (1.0 + (a - 1.0) * ka_ref[...])

    ti = lax.broadcasted_iota(jnp.int32, (2 * c_len, c_len), 0)
    si = lax.broadcasted_iota(jnp.int32, (2 * c_len, c_len), 1)
    tri = jnp.where((si <= ti) | (ti >= c_len), 1.0, 0.0).astype(BF16)
    tri2 = jnp.concatenate([tri] * 2, axis=1)
    cums, totals = [], []
    for ci in range(n_chunks):
        parts = _bf16_pieces(logw[ci * c_len:(ci + 1) * c_len], 2)
        cc = jnp.dot(tri2, jnp.concatenate(parts, axis=0), preferred_element_type=F32)
        cums.append(cc[:c_len])
        totals.append(cc[c_len:])
    cum = jnp.concatenate(cums, axis=0)
    total = jnp.concatenate(totals, axis=0)
    w_inv = jnp.exp(-cum)
    w_tail = jnp.exp(total - cum)
    w_end = jnp.exp(total)
    beta = kk * a
    al_t = (-kk * jnp.exp(cum - logw)).astype(BF16)
    r_t = (r * jnp.exp(cum)).astype(BF16)
    be_t = (beta * w_inv).astype(BF16)
    k_t = (k * w_inv).astype(BF16)
    be_e = (beta * w_tail).astype(BF16)
    k_e = (k * w_tail).astype(BF16)
    v_b = v.astype(BF16)

    n2 = 2 * c_len
    ii = lax.broadcasted_iota(jnp.int32, (n2, n2), 0)
    jj = lax.broadcasted_iota(jnp.int32, (n2, n2), 1)
    as_mask = lambda cond: jnp.where(cond, 1.0, 0.0).astype(BF16)
    strict_b = as_mask(jj < ii)
    incl_b = as_mask(jj <= ii)
    eye = jnp.where(ii == jj, 1.0, 0.0).astype(F32)
    level_masks = []
    blk = 2
    while blk <= c_len:
        half = blk // 2
        m = (ii // blk == jj // blk) & (ii % blk >= half) & (jj % blk < half)
        level_masks.append(as_mask(m))
        blk *= 2
    masks = (strict_b, incl_b, eye, level_masks)

    keys, insts = [], []
    for ci in range(n_chunks):
        rows = slice(ci * c_len, (ci + 1) * c_len)
        for j in range(n_pairs):
            sl = slice(j * LANES, (j + 1) * LANES)
            keys.append((ci, j))
            insts.append(tuple(_stack_heads(t[rows, sl]) for t in (al_t, r_t, be_t, k_t, v_b, be_e, k_e)))
    terms = dict(zip(keys, _rwkv_chunk_terms(insts, masks)))

    y_rows = []
    states = [state_ref[j] for j in range(n_pairs)]
    for ci in range(n_chunks):
        ys = []
        for j in range(n_pairs):
            rq, yq, pmat, qmat = terms[ci, j]
            s_b = states[j].astype(BF16)
            y_s = _dot_nt(rq, s_b) + yq
            ys.append(y_s[:c_len] + y_s[c_len:])
            w_row = w_end[ci * c_len:ci * c_len + 1, j * LANES:(j + 1) * LANES]
            states[j] = states[j] * w_row + _dot(s_b, pmat) + qmat
        y_rows.append(jnp.concatenate(ys, axis=1))
    for j in range(n_pairs):
        state_ref[j] = states[j]

    y = jnp.concatenate(y_rows, axis=0)
    mean = _group_sum(y) * (1.0 / HEAD_DIM)
    yc = y - mean
    var = _group_sum(yc * yc) * (1.0 / HEAD_DIM)
    yn = yc * lax.rsqrt(var + LNX_EPS) * lng_ref[...] + lnb_ref[...]
    bonus = _group_sum(r * k * rk_ref[...]) * v
    o_ref[0] = ((yn + bonus) * _silu(z_ref[0])).astype(o_ref.dtype)


def rwkv_call(p_a, z, mu, w0, w_up, a0, a_up, k_k, k_a, r_k, lnx_g, lnx_b):
    bsz, t_len, _ = p_a.shape
    t_blk = min(RBLOCK, t_len)
    assert t_len % t_blk == 0 and t_blk % RCHUNK == 0
    lora_w = jnp.zeros((2 * LORA, 2 * A_WIDTH), F32)
    lora_w = lora_w.at[:LORA, :A_WIDTH].set(w_up).at[LORA:, A_WIDTH:].set(a_up)
    row = lambda a: a.reshape(1, -1)
    const = lambda shape: pl.BlockSpec(shape, lambda b, i: (0,) * len(shape))
    return pl.pallas_call(
        _rwkv_kernel,
        out_shape=jax.ShapeDtypeStruct((bsz, t_len, A_WIDTH), BF16),
        grid=(bsz, t_len // t_blk),
        in_specs=[
            pl.BlockSpec((1, t_blk, SHIFT_W), lambda b, i: (b, i, 0)),
            pl.BlockSpec((1, t_blk, A_WIDTH), lambda b, i: (b, i, 0)),
            const((1, SHIFT_W)), const((1, A_WIDTH)), const((1, A_WIDTH)),
            const((2 * LORA, 2 * A_WIDTH)),
            const((1, A_WIDTH)), const((1, A_WIDTH)), const((1, A_WIDTH)),
            const((1, A_WIDTH)), const((1, A_WIDTH)),
        ],
        out_specs=pl.BlockSpec((1, t_blk, A_WIDTH), lambda b, i: (b, i, 0)),
        scratch_shapes=[
            pltpu.VMEM((A_WIDTH // LANES, LANES, LANES), F32),
            pltpu.VMEM((8, SHIFT_W), F32),
        ],
        compiler_params=_cparams(("arbitrary", "arbitrary")),
        name="rwkv7",
    )(p_a, z, row(mu), row(w0), row(a0), lora_w, row(k_k), row(k_a), row(r_k),
      row(lnx_g), row(lnx_b))


def _gmlp_kernel(u_ref, v_ref, z_ref, lng_ref, lnb_ref, w_ref, bias_ref, o_ref):
    v = v_ref[0]
    n_chunks = v.shape[0] // CHUNK
    mean = _group_sum(v) * (1.0 / HEAD_DIM)
    vc = v - mean
    var = _group_sum(vc * vc) * (1.0 / HEAD_DIM)
    vn = (vc * lax.rsqrt(var + GMLP_LN_EPS) * lng_ref[...] + lnb_ref[...]).astype(BF16)
    ti = lax.broadcasted_iota(jnp.int32, (CHUNK, CHUNK), 0)
    si = lax.broadcasted_iota(jnp.int32, (CHUNK, CHUNK), 1)
    causal = si <= ti
    w_pairs = []
    for j in range(B_WIDTH // LANES):
        w0 = jnp.where(causal, w_ref[2 * j], 0.0)
        w1 = jnp.where(causal, w_ref[2 * j + 1], 0.0)
        w_pairs.append(jnp.concatenate([w0, w1], axis=1).astype(BF16))
    rows = []
    for ci in range(n_chunks):
        mixed = []
        for j, w_pair in enumerate(w_pairs):
            vn_s = _stack_heads(vn[ci * CHUNK:(ci + 1) * CHUNK, j * LANES:(j + 1) * LANES])
            mixed.append(_dot(w_pair, vn_s))
        rows.append(jnp.concatenate(mixed, axis=1) + bias_ref[...])
    mixed = jnp.concatenate(rows, axis=0)
    o_ref[0] = (u_ref[0] * mixed * _silu(z_ref[0])).astype(o_ref.dtype)


def gmlp_call(u, v, z, ln_g, ln_b, w_s, b_s):
    bsz, t_len, _ = u.shape
    groups = w_s.shape[0]
    bias = jnp.repeat(b_s.T, HEAD_DIM, axis=1)
    row = lambda a: a.reshape(1, -1)
    const = lambda shape: pl.BlockSpec(shape, lambda b, i: (0,) * len(shape))
    tg = min(GBLOCK, t_len)
    assert t_len % tg == 0 and tg % CHUNK == 0
    blk = pl.BlockSpec((1, tg, B_WIDTH), lambda b, i: (b, i, 0))
    return pl.pallas_call(
        _gmlp_kernel,
        out_shape=jax.ShapeDtypeStruct((bsz, t_len, B_WIDTH), BF16),
        grid=(bsz, t_len // tg),
        in_specs=[
            blk, blk,
            pl.BlockSpec((1, tg, B_WIDTH), lambda b, i: (b, i, 1)),
            const((1, B_WIDTH)), const((1, B_WIDTH)),
            const((groups, CHUNK, CHUNK)), const((CHUNK, B_WIDTH)),
        ],
        out_specs=blk,
        compiler_params=_cparams(("arbitrary", "arbitrary")),
        name="gmlp",
    )(u, v, z, row(ln_g), row(ln_b), w_s, bias)


def _swa_bias_table():
    w = WINDOW
    slopes = (2.0 ** (-8.0 * np.arange(1, C_HEADS + 1, dtype=np.float32) / C_HEADS)).astype(np.float64)
    dist = (np.arange(w)[:, None] + w - np.arange(2 * w)[None, :]).astype(np.float64)
    valid = (dist >= 0) & (dist < w)
    table = np.empty((2, C_HEADS // 2, 2 * w, 2 * w), np.float32)
    for variant in range(2):
        ok = valid & ((np.arange(2 * w)[None, :] >= w) | (variant == 1))
        for h in range(C_HEADS):
            rows = slice((h % 2) * w, (h % 2 + 1) * w)
            table[variant, h // 2, rows] = np.where(ok, -slopes[h] * dist * LOG2E, -np.inf)
    return table


def _swa_kernel(sink_ref, q_ref, kp_ref, kc_ref, vp_ref, vc_ref, z_ref, bfirst_ref, brest_ref, o_ref):
    w = WINDOW
    n_win = q_ref.shape[1] // w
    group = C_HEADS // C_KV_HEADS
    kall = jnp.concatenate([kp_ref[0], kc_ref[0]], axis=0)
    vall = jnp.concatenate([vp_ref[0], vc_ref[0]], axis=0)
    first = lax.broadcasted_iota(jnp.int32, kall.shape, 1) < HEAD_DIM
    kroll = pltpu.roll(kall, HEAD_DIM, 1)
    vroll = pltpu.roll(vall, HEAD_DIM, 1)
    upper = lax.broadcasted_iota(jnp.int32, (2 * w, 1), 0) < w
    lane_first = lax.broadcasted_iota(jnp.int32, (w, LANES), 1) < HEAD_DIM
    qs = q_ref[0] * (HEAD_DIM ** -0.5 * LOG2E)

    outs = [[] for _ in range(n_win)]
    for kv in range(C_KV_HEADS):
        if kv == 0:
            kdup = jnp.where(first, kall, kroll).astype(BF16)
            vdup = jnp.where(first, vall, vroll).astype(BF16)
        else:
            kdup = jnp.where(first, kroll, kall).astype(BF16)
            vdup = jnp.where(first, vroll, vall).astype(BF16)
        for pi in range(group // 2):
            h0 = kv * group + 2 * pi
            sink = jnp.where(upper, sink_ref[h0] * LOG2E, sink_ref[h0 + 1] * LOG2E)
            for s in range(n_win):
                bias_ref = bfirst_ref if s == 0 else brest_ref
                q_s = _stack_heads(qs[s * w:(s + 1) * w, h0 * HEAD_DIM:(h0 + 2) * HEAD_DIM]).astype(BF16)
                sc = _dot_nt(q_s, kdup[s * w:(s + 2) * w]) + bias_ref[0, h0 // 2]
                m = jnp.maximum(jnp.max(sc, axis=-1, keepdims=True), sink)
                e = jnp.exp2(sc - m)
                denom = jnp.sum(e, axis=-1, keepdims=True) + jnp.exp2(sink - m)
                o = _dot(e.astype(BF16), vdup[s * w:(s + 2) * w]) * (1.0 / denom)
                outs[s].append(jnp.where(lane_first, o[:w], o[w:]))
    out = jnp.concatenate([jnp.concatenate(o, axis=1) for o in outs], axis=0)
    o_ref[0] = (out * _silu(z_ref[0])).astype(o_ref.dtype)


def swa_call(q, k, v, z, sinks):
    bsz, t_len, mix = q.shape
    qb = min(QBLOCK, t_len)
    n_win = qb // WINDOW
    assert t_len % qb == 0 and qb % WINDOW == 0 and mix == C_HEADS * HEAD_DIM
    kv_w = k.shape[-1]
    table = jnp.asarray(_swa_bias_table())
    cur = lambda b, i, *_: (b, i, 0)
    prev = lambda b, i, *_: (b, jnp.maximum(i * n_win - 1, 0), 0)
    bias_blk = (1,) + table.shape[1:]
    grid_spec = pltpu.PrefetchScalarGridSpec(
        num_scalar_prefetch=1,
        grid=(bsz, t_len // qb),
        in_specs=[
            pl.BlockSpec((1, qb, mix), cur),
            pl.BlockSpec((1, WINDOW, kv_w), prev), pl.BlockSpec((1, qb, kv_w), cur),
            pl.BlockSpec((1, WINDOW, kv_w), prev), pl.BlockSpec((1, qb, kv_w), cur),
            pl.BlockSpec((1, qb, mix), cur),
            pl.BlockSpec(bias_blk, lambda b, i, *_: (jnp.minimum(i, 1), 0, 0, 0)),
            pl.BlockSpec(bias_blk, lambda b, i, *_: (1, 0, 0, 0)),
        ],
        out_specs=pl.BlockSpec((1, qb, mix), cur),
    )
    return pl.pallas_call(
        _swa_kernel,
        out_shape=jax.ShapeDtypeStruct((bsz, t_len, mix), BF16),
        grid_spec=grid_spec,
        compiler_params=_cparams(("arbitrary", "arbitrary")),
        name="swa",
    )(sinks, q, k, k, v, v, z, table, table)


def _out_proj_kernel(*refs, n_y, final):
    y_refs = refs[:n_y]
    w_ref, x_ref, gate_ref = refs[n_y:n_y + 3]
    rest = refs[n_y + 3:]
    acc = None
    off = 0
    for y_ref in y_refs:
        wd = y_ref.shape[-1]
        part = jnp.dot(y_ref[0], w_ref[off:off + wd, :], preferred_element_type=F32)
        acc = part if acc is None else acc + part
        off += wd
    x = x_ref[0] + gate_ref[0] * acc
    if final:
        g_ref, o_ref = rest
        r = lax.rsqrt(jnp.mean(x * x, axis=-1, keepdims=True) + NORM_EPS)
        o_ref[0] = (x * r) * g_ref[...]
    else:
        (o_ref,) = rest
        o_ref[0] = x


def out_proj_call(ys, w_bf16, x, gate, final_g=None, tm=512):
    bsz, t_len, d = x.shape
    final = final_g is not None
    tm = min(tm, t_len)
    assert t_len % tm == 0
    in_specs = [pl.BlockSpec((1, tm, y.shape[-1]), lambda b, i: (b, i, 0)) for y in ys]
    in_specs += [
        pl.BlockSpec(w_bf16.shape, lambda b, i: (0, 0)),
        pl.BlockSpec((1, tm, d), lambda b, i: (b, i, 0)),
        pl.BlockSpec((1, 1, d), lambda b, i: (b, 0, 0)),
    ]
    args = list(ys) + [w_bf16, x, gate]
    if final:
        in_specs.append(pl.BlockSpec((1, d), lambda b, i: (0, 0)))
        args.append(final_g.reshape(1, d))
    return pl.pallas_call(
        functools.partial(_out_proj_kernel, n_y=len(ys), final=final),
        out_shape=jax.ShapeDtypeStruct((bsz, t_len, d), F32),
        grid=(bsz, t_len // tm),
        in_specs=in_specs,
        out_specs=pl.BlockSpec((1, tm, d), lambda b, i: (b, i, 0)),
        compiler_params=_cparams(("arbitrary", "arbitrary")),
        name="out_proj_final" if final else "out_proj",
    )(*args)


def kernel(x, c, ada_w, ada_b, norm_g, e_w_in, e_mu, e_w0, e_w_up, e_a0, e_a_up, e_k_k, e_k_a, e_r_k,
           e_lnx_g, e_lnx_b, e_sg_ln_g, e_sg_ln_b, e_sg_w, e_sg_b, e_w_out, o_w_in, o_sinks, o_w_out,
           final_g):
    depth = ada_w.shape[0]
    d = x.shape[-1]
    mod = adaln_call(c, ada_w, ada_b)
    for i in range(depth):
        shift = mod[i, :, None, 0:d]
        scale = mod[i, :, None, d:2 * d]
        gate = mod[i, :, None, 2 * d:3 * d]
        j = i // 2
        last = i == depth - 1
        if i % 2 == 0:
            p_a, u, v, z = in_proj_call(x, norm_g[i], scale, shift, e_w_in[j].astype(BF16),
                                        (SHIFT_W, B_WIDTH, B_WIDTH, A_WIDTH + B_WIDTH))
            y_a = rwkv_call(p_a, z, e_mu[j], e_w0[j], e_w_up[j], e_a0[j], e_a_up[j], e_k_k[j], e_k_a[j],
                            e_r_k[j].reshape(-1), e_lnx_g[j], e_lnx_b[j])
            y_b = gmlp_call(u, v, z, e_sg_ln_g[j], e_sg_ln_b[j], e_sg_w[j], e_sg_b[j])
            ys, w_out = (y_a, y_b), e_w_out[j]
        else:
            mix = o_w_out.shape[1]
            kv_w = C_KV_HEADS * HEAD_DIM
            q, k, v, z = in_proj_call(x, norm_g[i], scale, shift, o_w_in[j].astype(BF16),
                                      (mix, kv_w, kv_w, mix))
            ys, w_out = (swa_call(q, k, v, z, o_sinks[j]),), o_w_out[j]
        x = out_proj_call(ys, w_out.astype(BF16), x, gate, final_g if last else None)
    return x
```
